```python
import jax, jax.numpy as jnp
from jax import lax
import numpy as np

D_MODEL = 2048
BATCH = 2
SEQ = 4096
DEPTH = 1
DEC_BATCH = 8
DEC_SEQ = 8
PAST_LEN = 16384
PAGE_SIZE = 128

DILATED_GROUPS = ((128, 1), (512, 4), (2048, 16))
N_GROUPS = len(DILATED_GROUPS)
HEAD_DIM = 128
N_HEADS = D_MODEL // (2 * HEAD_DIM)
ATTN_WIDTH = N_GROUPS * N_HEADS * HEAD_DIM
ATTN_OUT = N_HEADS * HEAD_DIM
D_CONV = 3 * D_MODEL // 4
CONV_K = 31
D_IN = 3 * ATTN_WIDTH + 2 * D_CONV + 2 * D_MODEL
N_EXPERTS = 32
TOP_K = 4
D_FF = D_MODEL
SWIGLU_ALPHA = 1.702
SWIGLU_LIMIT = 7.0
N_MOD = 6
Q_BLOCK = 128
NORM_EPS = 1e-6
NEG_INF = -1e30

kernel_name = 'hybrid_dilated_attn_conformer_moe_decode_step'


def _rmsnorm(x, g):
    xf = x.astype(jnp.float32)
    y = xf * lax.rsqrt(jnp.mean(xf * xf, axis=-1, keepdims=True) + NORM_EPS)
    return (y * g.astype(jnp.float32)).astype(x.dtype)


def _layernorm(x, g, b):
    xf = x.astype(jnp.float32)
    mu = jnp.mean(xf, axis=-1, keepdims=True)
    xc = xf - mu
    var = jnp.mean(xc * xc, axis=-1, keepdims=True)
    y = xc * lax.rsqrt(var + NORM_EPS) * g.astype(jnp.float32) + b.astype(jnp.float32)
    return y.astype(x.dtype)


def _adaln(c, w_ada, b_ada):
    return jnp.split(jax.nn.silu(c) @ w_ada + b_ada, N_MOD, axis=-1)


def _modulate(h, shift, scale):
    return h * (1 + scale[:, None, :]) + shift[:, None, :]


def _split_in(z):
    cuts = [ATTN_WIDTH, 2 * ATTN_WIDTH, 3 * ATTN_WIDTH, 3 * ATTN_WIDTH + 2 * D_CONV]
    q, k, v, u2, gates = jnp.split(z, cuts, axis=-1)
    heads = lambda t: t.reshape(t.shape[0], t.shape[1], N_GROUPS, N_HEADS, HEAD_DIM)
    ua, ub = jnp.split(u2, 2, axis=-1)
    return heads(q), heads(k), heads(v), ua * jax.nn.sigmoid(ub), gates


def _dilated_group(q, k_ctx, v_ctx, pos, ctx_start, window, dilation):
    offsets = jnp.arange(window // dilation + 1, dtype=jnp.int32) * dilation
    key_pos = pos[:, None] - offsets[None, :]
    valid = key_pos >= 0
    idx = jnp.clip(key_pos - ctx_start, 0, k_ctx.shape[1] - 1)
    k = jnp.take(k_ctx, idx, axis=1)
    v = jnp.take(v_ctx, idx, axis=1)
    logits = jnp.einsum('bqhd,bqjhd->bqhj', q, k, preferred_element_type=jnp.float32) * (HEAD_DIM ** -0.5)
    return jnp.where(valid[None, :, None, :], logits, NEG_INF), v


def _dilated_mixture(q, kv_ctx, pos, starts):
    m = jnp.full(q.shape[:2] + (N_HEADS,), NEG_INF, jnp.float32)
    den = jnp.zeros(q.shape[:2] + (N_HEADS,), jnp.float32)
    num = jnp.zeros(q.shape[:2] + (N_HEADS, HEAD_DIM), jnp.float32)
    for g, ((window, dilation), (k_ctx, v_ctx), start) in enumerate(zip(DILATED_GROUPS, kv_ctx, starts)):
        logits, v = _dilated_group(q[:, :, g], k_ctx, v_ctx, pos, start, window, dilation)
        m_new = jnp.maximum(m, jnp.max(logits, axis=-1))
        p = jnp.exp(logits - m_new[..., None])
        corr = jnp.exp(m - m_new)
        den = den * corr + jnp.sum(p, axis=-1)
        num = num * corr[..., None] + jnp.einsum('bqhj,bqjhd->bqhd', p, v, preferred_element_type=jnp.float32)
        m = m_new
    return num / den[..., None]


def _prompt_attention(q, k, v):
    b, s = q.shape[:2]
    n_blocks = s // Q_BLOCK
    kv_ctx = [(k[:, :, g], v[:, :, g]) for g in range(N_GROUPS)]
    starts = [0] * N_GROUPS
    q_blocks = q.reshape(b, n_blocks, Q_BLOCK, N_GROUPS, N_HEADS, HEAD_DIM).swapaxes(0, 1)
    pos_blocks = jnp.arange(s, dtype=jnp.int32).reshape(n_blocks, Q_BLOCK)
    o = lax.map(lambda qp: _dilated_mixture(qp[0], kv_ctx, qp[1], starts), (q_blocks, pos_blocks))
    return o.swapaxes(0, 1).reshape(b, s, ATTN_OUT).astype(q.dtype)


def _sample_attention(q, k, v, caches):
    pos = PAST_LEN + jnp.arange(q.shape[1], dtype=jnp.int32)
    kv_ctx, starts = [], []
    for g, cache in enumerate(caches):
        kv_ctx.append((jnp.concatenate([cache[:, :, 0], k[:, :, g]], axis=1),
                       jnp.concatenate([cache[:, :, 1], v[:, :, g]], axis=1)))
        starts.append(PAST_LEN - cache.shape[1])
    o = _dilated_mixture(q, kv_ctx, pos, starts)
    return o.reshape(q.shape[0], q.shape[1], ATTN_OUT).astype(q.dtype)


def _conv_branch(ctx, w_dw, b_dw, g_ln, b_ln):
    y = lax.conv_general_dilated(ctx, w_dw[:, None, :], window_strides=(1,), padding='VALID',
                                 dimension_numbers=('NWC', 'WIO', 'NWC'),
                                 feature_group_count=ctx.shape[-1]) + b_dw
    return jax.nn.silu(_layernorm(y, g_ln, b_ln))


def _merge(attn_o, conv_f, gates, w_attn_out, b_attn_out, w_conv_out, b_conv_out, w_o):
    a = attn_o @ w_attn_out + b_attn_out
    cb = conv_f @ w_conv_out + b_conv_out
    ga, gb = jnp.split(gates, 2, axis=-1)
    return (jax.nn.sigmoid(ga) * a + jax.nn.sigmoid(gb) * cb) @ w_o


def _moe(h, w_router, b_router, w1, b1, w2, b2):
    logits = (h @ w_router + b_router).astype(jnp.float32)
    top_val, top_idx = lax.top_k(logits, TOP_K)
    top_w = jax.nn.softmax(top_val, axis=-1)
    gate = jnp.sum(jax.nn.one_hot(top_idx, N_EXPERTS, dtype=jnp.float32) * top_w[..., None], axis=1)
    out = jnp.zeros(h.shape, jnp.float32)
    for e in range(N_EXPERTS):
        z = h @ w1[e] + b1[e]
        z_glu = jnp.minimum(z[:, :D_FF], SWIGLU_LIMIT)
        z_lin = jnp.clip(z[:, D_FF:], -SWIGLU_LIMIT, SWIGLU_LIMIT)
        act = z_glu * jax.nn.sigmoid(SWIGLU_ALPHA * z_glu) * (z_lin + 1)
        out = out + gate[:, e:e + 1] * (act @ w2[e] + b2[e])
    return out.astype(h.dtype)


def setup_inputs(seed: int = 0) -> dict:
    key = jax.random.key(seed)
    ks = iter(jax.random.split(key, 40))
    nrm = lambda shape, s: s * jax.random.normal(next(ks), shape, jnp.float32)
    L, D = DEPTH, D_MODEL
    inp = {}
    inp['x_prompt'] = nrm((BATCH, SEQ, D), 1.0)
    inp['x_sample'] = nrm((DEC_BATCH, DEC_SEQ, D), 1.0)
    for w, _ in DILATED_GROUPS:
        inp['cache_kv_w%d' % w] = nrm((L, DEC_BATCH, min(w, PAST_LEN), 2, N_HEADS, HEAD_DIM), 1.0)
    inp['state_conv'] = nrm((L, DEC_BATCH, CONV_K - 1, D_CONV), 0.5)
    inp['c_prompt'] = nrm((BATCH, D), 1.0)
    inp['c_sample'] = nrm((DEC_BATCH, D), 1.0)
    inp['w_ada'] = nrm((L, D, N_MOD * D), 0.5 * D ** -0.5)
    inp['b_ada'] = nrm((L, N_MOD * D), 0.02)
    inp['g_norm1'] = 1.0 + nrm((L, D), 0.05)
    inp['w_in'] = nrm((L, D, D_IN), D ** -0.5)
    inp['b_in'] = nrm((L, D_IN), 0.02)
    inp['w_dw'] = nrm((L, CONV_K, D_CONV), CONV_K ** -0.5)
    inp['b_dw'] = nrm((L, D_CONV), 0.02)
    inp['g_ln_conv'] = 1.0 + nrm((L, D_CONV), 0.05)
    inp['b_ln_conv'] = nrm((L, D_CONV), 0.02)
    inp['w_conv_out'] = nrm((L, D_CONV, D), D_CONV ** -0.5)
    inp['b_conv_out'] = nrm((L, D), 0.02)
    inp['w_attn_out'] = nrm((L, ATTN_OUT, D), ATTN_OUT ** -0.5)
    inp['b_attn_out'] = nrm((L, D), 0.02)
    inp['w_o'] = nrm((L, D, D), D ** -0.5)
    inp['g_norm2'] = 1.0 + nrm((L, D), 0.05)
    inp['w_router'] = nrm((L, D, N_EXPERTS), D ** -0.5)
    inp['b_router'] = nrm((L, N_EXPERTS), 0.01)
    inp['w_moe1'] = nrm((L, N_EXPERTS, D, 2 * D_FF), D ** -0.5)
    inp['b_moe1'] = nrm((L, N_EXPERTS, 2 * D_FF), 0.02)
    inp['w_moe2'] = nrm((L, N_EXPERTS, D_FF, D), D_FF ** -0.5)
    inp['b_moe2'] = nrm((L, N_EXPERTS, D), 0.02)
    inp['g_final'] = 1.0 + nrm((D,), 0.05)
    return inp


def reference(x_prompt, x_sample, cache_kv_w128, cache_kv_w512, cache_kv_w2048, state_conv,
              c_prompt, c_sample, w_ada, b_ada, g_norm1, w_in, b_in, w_dw, b_dw, g_ln_conv, b_ln_conv,
              w_conv_out, b_conv_out, w_attn_out, b_attn_out, w_o, g_norm2, w_router, b_router,
              w_moe1, b_moe1, w_moe2, b_moe2, g_final):
    xp, xs = x_prompt, x_sample
    caches = (cache_kv_w128, cache_kv_w512, cache_kv_w2048)
    n_p = xp.shape[0] * xp.shape[1]
    new_kv_p = [[] for _ in DILATED_GROUPS]
    new_kv_s = [[] for _ in DILATED_GROUPS]
    new_conv_p, new_conv_s = [], []
    for l in range(DEPTH):
        mod_p = _adaln(c_prompt, w_ada[l], b_ada[l])
        mod_s = _adaln(c_sample, w_ada[l], b_ada[l])
        hp = _modulate(_rmsnorm(xp, g_norm1[l]), mod_p[0], mod_p[1])
        hs = _modulate(_rmsnorm(xs, g_norm1[l]), mod_s[0], mod_s[1])
        qp, kp, vp, up, gp = _split_in(hp @ w_in[l] + b_in[l])
        qs, ks, vs, us, gs = _split_in(hs @ w_in[l] + b_in[l])
        ap = _prompt_attention(qp, kp, vp)
        a_s = _sample_attention(qs, ks, vs, [c[l] for c in caches])
        ctx_p = jnp.concatenate([jnp.zeros((up.shape[0], CONV_K - 1, D_CONV), up.dtype), up], axis=1)
        ctx_s = jnp.concatenate([state_conv[l], us], axis=1)
        fp = _conv_branch(ctx_p, w_dw[l], b_dw[l], g_ln_conv[l], b_ln_conv[l])
        fs = _conv_branch(ctx_s, w_dw[l], b_dw[l], g_ln_conv[l], b_ln_conv[l])
        for g, (w, _) in enumerate(DILATED_GROUPS):
            new_kv_p[g].append(jnp.stack([kp[:, :, g], vp[:, :, g]], axis=2)[:, -min(w, kp.shape[1]):])
            new_kv_s[g].append(jnp.stack([ks[:, :, g], vs[:, :, g]], axis=2))
        new_conv_p.append(ctx_p[:, -(CONV_K - 1):])
        new_conv_s.append(ctx_s[:, -(CONV_K - 1):])
        xp = xp + mod_p[2][:, None, :] * _merge(ap, fp, gp, w_attn_out[l], b_attn_out[l], w_conv_out[l], b_conv_out[l], w_o[l])
        xs = xs + mod_s[2][:, None, :] * _merge(a_s, fs, gs, w_attn_out[l], b_attn_out[l], w_conv_out[l], b_conv_out[l], w_o[l])
        hp = _modulate(_rmsnorm(xp, g_norm2[l]), mod_p[3], mod_p[4])
        hs = _modulate(_rmsnorm(xs, g_norm2[l]), mod_s[3], mod_s[4])
        h_all = jnp.concatenate([hp.reshape(-1, D_MODEL), hs.reshape(-1, D_MODEL)], axis=0)
        m = _moe(h_all, w_router[l], b_router[l], w_moe1[l], b_moe1[l], w_moe2[l], b_moe2[l])
        xp = xp + mod_p[5][:, None, :] * m[:n_p].reshape(xp.shape)
        xs = xs + mod_s[5][:, None, :] * m[n_p:].reshape(xs.shape)
    y_prompt = _rmsnorm(xp, g_final)
    y_sample = _rmsnorm(xs, g_final)
    return (y_prompt, y_sample,
            jnp.stack(new_kv_p[0]), jnp.stack(new_kv_p[1]), jnp.stack(new_kv_p[2]), jnp.stack(new_conv_p),
            jnp.stack(new_kv_s[0]), jnp.stack(new_kv_s[1]), jnp.stack(new_kv_s[2]), jnp.stack(new_conv_s))
```

```python
import functools

import jax
import jax.numpy as jnp
from jax import lax
from jax.experimental import pallas as pl
from jax.experimental.pallas import tpu as pltpu

DILATED_GROUPS = ((128, 1), (512, 4), (2048, 16))
N_GROUPS = len(DILATED_GROUPS)
KEYS_PER_GROUP = 128
CONV_K = 31
TOP_K = 4
N_MOD = 6
SWIGLU_ALPHA = 1.702
SWIGLU_LIMIT = 7.0
NORM_EPS = 1e-6
NEG_INF = -1e30

LANES = 128
VMEM_LIMIT_BYTES = 56 * 1024 * 1024

CONV_HALO = 32
EXPERT_TILE = 256
DISPATCH_TILE = 192

F32 = jnp.float32
BF16 = jnp.bfloat16
HIGHEST = lax.Precision.HIGHEST


def _params(n_axes):
    return pltpu.CompilerParams(dimension_semantics=("arbitrary",) * n_axes,
                                vmem_limit_bytes=VMEM_LIMIT_BYTES)


def _dot(a, b, precise):
    if precise:
        return jnp.dot(a, b, precision=HIGHEST, preferred_element_type=F32)
    return jnp.dot(a, b, preferred_element_type=F32)


def _dot_t(a, b, precise):
    dn = (((1,), (1,)), ((), ()))
    if precise:
        return lax.dot_general(a, b, dn, precision=HIGHEST, preferred_element_type=F32)
    return lax.dot_general(a, b, dn, preferred_element_type=F32)


def _rms(x):
    return x * lax.rsqrt(jnp.mean(x * x, axis=-1, keepdims=True) + NORM_EPS)


def _adaln_body(c_ref, w_ref, b_ref, o_ref):
    c = c_ref[...]
    s = c * jax.nn.sigmoid(c)
    o_ref[...] = _dot(s, w_ref[...], True) + b_ref[...]


def _adaln(c_rows, w_ada, b_ada, tn=1024):
    r, d = c_rows.shape
    n = w_ada.shape[1]
    return pl.pallas_call(
        _adaln_body,
        out_shape=jax.ShapeDtypeStruct((r, n), F32),
        grid=(n // tn,),
        in_specs=[pl.BlockSpec((r, d), lambda j: (0, 0)),
                  pl.BlockSpec((d, tn), lambda j: (0, j)),
                  pl.BlockSpec((1, tn), lambda j: (0, j))],
        out_specs=pl.BlockSpec((r, tn), lambda j: (0, j)),
        compiler_params=_params(1),
        name="adaln",
    )(c_rows, w_ada, b_ada.reshape(1, n))


def _prenorm_body(x_ref, g_ref, shift_ref, scale_ref, o_ref):
    y = _rms(x_ref[...]) * g_ref[...]
    o_ref[...] = (y * (1.0 + scale_ref[0]) + shift_ref[0]).astype(o_ref.dtype)


def _prenorm(x2d, g, mods3, mod_row0, rows_per_seq, tm, shift_chunk, out_dtype):
    m, d = x2d.shape
    tiles_per_seq = rows_per_seq // tm
    row = lambda i: mod_row0 + i // tiles_per_seq
    return pl.pallas_call(
        _prenorm_body,
        out_shape=jax.ShapeDtypeStruct((m, d), out_dtype),
        grid=(m // tm,),
        in_specs=[pl.BlockSpec((tm, d), lambda i: (i, 0)),
                  pl.BlockSpec((1, d), lambda i: (0, 0)),
                  pl.BlockSpec((1, 1, d), lambda i: (row(i), 0, shift_chunk)),
                  pl.BlockSpec((1, 1, d), lambda i: (row(i), 0, shift_chunk + 1))],
        out_specs=pl.BlockSpec((tm, d), lambda i: (i, 0)),
        compiler_params=_params(1),
        name="prenorm",
    )(x2d, g.reshape(1, d), mods3, mods3)


def _inproj_body(*refs, mode, precise):
    n_w = 2 if mode == "glu" else 1
    x_ref = refs[0]
    w_refs = refs[1:1 + n_w]
    b_refs = refs[1 + n_w:1 + 2 * n_w]
    o_ref = refs[1 + 2 * n_w]
    w_scr = refs[2 + 2 * n_w:]
    if precise:
        ws = [w[...] for w in w_refs]
    else:
        @pl.when(pl.program_id(1) == 0)
        def _():
            for w, s in zip(w_refs, w_scr):
                s[...] = w[...].astype(BF16)
        ws = [s[...] for s in w_scr]
    x = x_ref[...]
    zs = [_dot(x, w, precise) + b[...] for w, b in zip(ws, b_refs)]
    if mode == "linear":
        out = zs[0]
    elif mode == "sigmoid":
        out = jax.nn.sigmoid(zs[0])
    else:
        out = zs[0] * jax.nn.sigmoid(zs[1])
    o_ref[...] = out.astype(o_ref.dtype)


def _inproj(x, w, b, col0, n_cols, mode, out_dtype, tm, tn, precise):
    m, k = x.shape
    n_w = 2 if mode == "glu" else 1
    cb0 = col0 // tn
    nb = n_cols // tn
    w_specs = [pl.BlockSpec((k, tn), functools.partial(lambda j, i, o: (0, o + j), o=cb0 + p * nb))
               for p in range(n_w)]
    b_specs = [pl.BlockSpec((1, tn), functools.partial(lambda j, i, o: (0, o + j), o=cb0 + p * nb))
               for p in range(n_w)]
    scratch = [] if precise else [pltpu.VMEM((k, tn), BF16) for _ in range(n_w)]
    return pl.pallas_call(
        functools.partial(_inproj_body, mode=mode, precise=precise),
        out_shape=jax.ShapeDtypeStruct((m, n_cols), out_dtype),
        grid=(nb, m // tm),
        in_specs=[pl.BlockSpec((tm, k), lambda j, i: (i, 0))] + w_specs + b_specs,
        out_specs=pl.BlockSpec((tm, tn), lambda j, i: (i, j)),
        scratch_shapes=scratch,
        compiler_params=_params(2),
        name="inproj_" + mode,
    )(x, *([w] * n_w), *([b] * n_w))


def _attn_group(q_ref, k_ref, v_ref, m_sc, l_sc, acc_sc, dilation, seq):
    blk = KEYS_PER_GROUP
    n_blocks = seq // dilation // blk
    scale = q_ref.shape[-1] ** -0.5
    row = lax.broadcasted_iota(jnp.int32, (blk, blk), 0)
    col = lax.broadcasted_iota(jnp.int32, (blk, blk), 1)
    mask_cur = col <= row
    mask_prev = col >= row

    def body(it, carry):
        r = it // n_blocks
        i = it % n_blocks
        q_rows = pl.ds(r + dilation * blk * i, blk, stride=dilation)
        p_rows = pl.ds(r + dilation * blk * jnp.maximum(i - 1, 0), blk, stride=dilation)
        q = q_ref[q_rows, :].astype(BF16)
        k_cur = k_ref[q_rows, :].astype(BF16)
        v_cur = v_ref[q_rows, :].astype(BF16)
        k_prev = k_ref[p_rows, :].astype(BF16)
        v_prev = v_ref[p_rows, :].astype(BF16)
        s_cur = jnp.where(mask_cur, _dot_t(q, k_cur, False) * scale, NEG_INF)
        s_prev = jnp.where(mask_prev & (i > 0), _dot_t(q, k_prev, False) * scale, NEG_INF)
        m_old = m_sc[q_rows, :]
        m_blk = jnp.maximum(jnp.max(s_cur, axis=-1, keepdims=True),
                            jnp.max(s_prev, axis=-1, keepdims=True))
        m_new = jnp.maximum(m_old, m_blk)
        p_cur = jnp.exp(s_cur - m_new)
        p_prev = jnp.exp(s_prev - m_new)
        corr = jnp.exp(m_old - m_new)
        l_new = l_sc[q_rows, :] * corr + (jnp.sum(p_cur, axis=-1, keepdims=True)
                                          + jnp.sum(p_prev, axis=-1, keepdims=True))
        pv = _dot(p_cur.astype(BF16), v_cur, False) + _dot(p_prev.astype(BF16), v_prev, False)
        acc_sc[q_rows, :] = acc_sc[q_rows, :] * corr + pv
        l_sc[q_rows, :] = l_new
        m_sc[q_rows, :] = m_new
        return carry

    lax.fori_loop(0, dilation * n_blocks, body, 0, unroll=2)


def _attn_prompt_body(q_ref, k_ref, v_ref, o_ref, m_sc, l_sc, acc_sc, *, seq):
    g = pl.program_id(2)

    @pl.when(g == 0)
    def _():
        m_sc[...] = jnp.full(m_sc.shape, NEG_INF, F32)
        l_sc[...] = jnp.zeros(l_sc.shape, F32)
        acc_sc[...] = jnp.zeros(acc_sc.shape, F32)

    for gi, (_, dilation) in enumerate(DILATED_GROUPS):
        @pl.when(g == gi)
        def _(dilation=dilation):
            _attn_group(q_ref, k_ref, v_ref, m_sc, l_sc, acc_sc, dilation, seq)

    @pl.when(g == N_GROUPS - 1)
    def _():
        o_ref[...] = (acc_sc[...] / l_sc[...]).astype(o_ref.dtype)


def _attn_prompt(z_qkv, batch, seq, n_heads, head_dim):
    gh = N_GROUPS * n_heads
    spec = lambda part: pl.BlockSpec((seq, head_dim),
                                     lambda b, h, g: (b, part * gh + g * n_heads + h))
    return pl.pallas_call(
        functools.partial(_attn_prompt_body, seq=seq),
        out_shape=jax.ShapeDtypeStruct((batch * seq, n_heads * head_dim), BF16),
        grid=(batch, n_heads, N_GROUPS),
        in_specs=[spec(0), spec(1), spec(2)],
        out_specs=pl.BlockSpec((seq, head_dim), lambda b, h, g: (b, h)),
        scratch_shapes=[pltpu.VMEM((seq, head_dim), F32)] * 3,
        compiler_params=_params(3),
        name="attn_prompt",
    )(z_qkv, z_qkv, z_qkv)


def _attn_sample_body(*refs, n_new, past_pad):
    q_refs, k_refs, v_refs = refs[0:3], refs[3:6], refs[6:9]
    cache_refs = refs[9:15]
    o_ref = refs[15]
    k_ctx, v_ctx = refs[16], refs[17]
    head_dim = o_ref.shape[-1]
    scale = head_dim ** -0.5
    m = jnp.full((n_new, 1), NEG_INF, F32)
    l = jnp.zeros((n_new, 1), F32)
    acc = jnp.zeros((n_new, head_dim), F32)
    for g, (window, dilation) in enumerate(DILATED_GROUPS):
        ck_ref, cv_ref = cache_refs[2 * g], cache_refs[2 * g + 1]
        n_ctx = window + past_pad
        k_ctx[0:window, :] = ck_ref[0]
        v_ctx[0:window, :] = cv_ref[0]
        k_ctx[window:window + n_new, :] = k_refs[g][0]
        v_ctx[window:window + n_new, :] = v_refs[g][0]
        k_ctx[window + n_new:n_ctx, :] = jnp.zeros((past_pad - n_new, head_dim), F32)
        v_ctx[window + n_new:n_ctx, :] = jnp.zeros((past_pad - n_new, head_dim), F32)
        q = q_refs[g][0]
        s = _dot_t(q, k_ctx[0:n_ctx, :], True) * scale
        qi = lax.broadcasted_iota(jnp.int32, (n_new, n_ctx), 0)
        ci = lax.broadcasted_iota(jnp.int32, (n_new, n_ctx), 1)
        dist = window + qi - ci
        valid = ((dist >= 0) & (dist <= window) & ((dist & (dilation - 1)) == 0)
                 & (ci < window + n_new))
        s = jnp.where(valid, s, NEG_INF)
        m_new = jnp.maximum(m, jnp.max(s, axis=-1, keepdims=True))
        p = jnp.exp(s - m_new)
        corr = jnp.exp(m - m_new)
        l = l * corr + jnp.sum(p, axis=-1, keepdims=True)
        acc = acc * corr + _dot(p, v_ctx[0:n_ctx, :], True)
        m = m_new
    o_ref[0] = acc / l


def _attn_sample(z_s, caches, n_seq, n_new, n_heads, head_dim):
    gh = N_GROUPS * n_heads
    z3 = z_s.reshape(n_seq, n_new, z_s.shape[-1])
    past_pad = LANES
    zspec = lambda part, g: pl.BlockSpec((1, n_new, head_dim),
                                         lambda b, h: (b, 0, part * gh + g * n_heads + h))
    in_specs = [zspec(part, g) for part in range(3) for g in range(N_GROUPS)]
    cache_args = []
    for g, (window, _) in enumerate(DILATED_GROUPS):
        c3 = caches[g].reshape(n_seq, window, 2 * n_heads * head_dim)
        for kv in range(2):
            in_specs.append(pl.BlockSpec((1, window, head_dim),
                                         functools.partial(lambda b, h, o: (b, 0, o + h), o=kv * n_heads)))
            cache_args.append(c3)
    w_max = max(w for w, _ in DILATED_GROUPS)
    out = pl.pallas_call(
        functools.partial(_attn_sample_body, n_new=n_new, past_pad=past_pad),
        out_shape=jax.ShapeDtypeStruct((n_seq, n_new, n_heads * head_dim), F32),
        grid=(n_seq, n_heads),
        in_specs=in_specs,
        out_specs=pl.BlockSpec((1, n_new, head_dim), lambda b, h: (b, 0, h)),
        scratch_shapes=[pltpu.VMEM((w_max + past_pad, head_dim), F32)] * 2,
        compiler_params=_params(2),
        name="attn_sample",
    )(*([z3] * 9), *cache_args)
    return out.reshape(n_seq * n_new, n_heads * head_dim)


def _conv_body(prev_ref, cur_ref, w_ref, b_ref, g_ref, bl_ref, o_ref, ctx_sc, y_sc, *,
               tt, zero_first, lane_chunk, time_chunk):
    n_ch = cur_ref.shape[-1]
    prev = prev_ref[0]
    if zero_first:
        prev = jnp.where(pl.program_id(1) == 0, 0.0, prev)
    ctx_sc[0:CONV_HALO, :] = prev
    ctx_sc[CONV_HALO:CONV_HALO + tt, :] = cur_ref[0]
    first_tap = CONV_HALO - (CONV_K - 1)
    for c0 in range(0, n_ch, lane_chunk):
        cols = slice(c0, c0 + lane_chunk)
        for t0 in range(0, tt, time_chunk):
            acc = jnp.zeros((time_chunk, lane_chunk), F32)
            for k in range(CONV_K):
                acc = acc + ctx_sc[t0 + first_tap + k:t0 + first_tap + k + time_chunk, cols] * w_ref[k:k + 1, cols]
            y_sc[t0:t0 + time_chunk, cols] = acc + b_ref[:, cols]
    y = y_sc[...]
    yc = y - jnp.mean(y, axis=-1, keepdims=True)
    var = jnp.mean(yc * yc, axis=-1, keepdims=True)
    out = yc * lax.rsqrt(var + NORM_EPS) * g_ref[...] + bl_ref[...]
    o_ref[0] = (out * jax.nn.sigmoid(out)).astype(o_ref.dtype)


def _conv_branch(u3, prev3, w_dw, b_dw, g_ln, b_ln, tt, prev_from_u, out_dtype):
    nb, length, n_ch = u3.shape
    halo_per_tile = tt // CONV_HALO
    if prev_from_u:
        prev_map = lambda b, i: (b, jnp.maximum(i * halo_per_tile - 1, 0), 0)
    else:
        prev_map = lambda b, i: (b, 0, 0)
    vec = lambda a: a.reshape(1, n_ch)
    vspec = pl.BlockSpec((1, n_ch), lambda b, i: (0, 0))
    time_chunk = min(tt, 64)
    return pl.pallas_call(
        functools.partial(_conv_body, tt=tt, zero_first=prev_from_u, lane_chunk=2 * LANES,
                          time_chunk=time_chunk),
        out_shape=jax.ShapeDtypeStruct((nb, length, n_ch), out_dtype),
        grid=(nb, length // tt),
        in_specs=[pl.BlockSpec((1, CONV_HALO, n_ch), prev_map),
                  pl.BlockSpec((1, tt, n_ch), lambda b, i: (b, i, 0)),
                  pl.BlockSpec((CONV_K, n_ch), lambda b, i: (0, 0)),
                  vspec, vspec, vspec],
        out_specs=pl.BlockSpec((1, tt, n_ch), lambda b, i: (b, i, 0)),
        scratch_shapes=[pltpu.VMEM((CONV_HALO + tt, n_ch), F32), pltpu.VMEM((tt, n_ch), F32)],
        compiler_params=_params(2),
        name="conv_branch",
    )(prev3, u3, w_dw, vec(b_dw), vec(g_ln), vec(b_ln))


def _merge1_body(ao_ref, cf_ref, ga_ref, gb_ref, wa_ref, wc_ref, ba_ref, bc_ref, o_ref, *w_scr, precise):
    if precise:
        wa, wc = wa_ref[...], wc_ref[...]
    else:
        @pl.when(pl.program_id(1) == 0)
        def _():
            w_scr[0][...] = wa_ref[...].astype(BF16)
            w_scr[1][...] = wc_ref[...].astype(BF16)
        wa, wc = w_scr[0][...], w_scr[1][...]
    a = _dot(ao_ref[...], wa, precise) + ba_ref[...]
    cb = _dot(cf_ref[...], wc, precise) + bc_ref[...]
    o_ref[...] = (ga_ref[...].astype(F32) * a + gb_ref[...].astype(F32) * cb).astype(o_ref.dtype)


def _merge1(attn_o, conv_f, gates, w_attn_out, b_attn_out, w_conv_out, b_conv_out, tm, tn, precise,
            out_dtype):
    m, ka = attn_o.shape
    kc = conv_f.shape[1]
    d = w_attn_out.shape[1]
    nb = d // tn
    scratch = [] if precise else [pltpu.VMEM((ka, tn), BF16), pltpu.VMEM((kc, tn), BF16)]
    return pl.pallas_call(
        functools.partial(_merge1_body, precise=precise),
        out_shape=jax.ShapeDtypeStruct((m, d), out_dtype),
        grid=(nb, m // tm),
        in_specs=[pl.BlockSpec((tm, ka), lambda j, i: (i, 0)),
                  pl.BlockSpec((tm, kc), lambda j, i: (i, 0)),
                  pl.BlockSpec((tm, tn), lambda j, i: (i, j)),
                  pl.BlockSpec((tm, tn), lambda j, i: (i, nb + j)),
                  pl.BlockSpec((ka, tn), lambda j, i: (0, j)),
                  pl.BlockSpec((kc, tn), lambda j, i: (0, j)),
                  pl.BlockSpec((1, tn), lambda j, i: (0, j)),
                  pl.BlockSpec((1, tn), lambda j, i: (0, j))],
        out_specs=pl.BlockSpec((tm, tn), lambda j, i: (i, j)),
        scratch_shapes=scratch,
        compiler_params=_params(2),
        name="merge_branches",
    )(attn_o, conv_f, gates, gates, w_attn_out, w_conv_out, b_attn_out.reshape(1, d),
      b_conv_out.reshape(1, d))


def _merge2_body(g_ref, x_ref, gate_ref, wo_ref, o_ref, *w_scr, precise):
    if precise:
        wo = wo_ref[...]
    else:
        @pl.when(pl.program_id(1) == 0)
        def _():
            w_scr[0][...] = wo_ref[...].astype(BF16)
        wo = w_scr[0][...]
    o_ref[...] = x_ref[...] + gate_ref[0] * _dot(g_ref[...], wo, precise)


def _merge2(g, x2d, mods3, mod_row0, rows_per_seq, w_o, tm, tn, precise):
    m, d = x2d.shape
    tiles_per_seq = rows_per_seq // tm
    gate_chunk0 = 2 * (d // tn)
    scratch = [] if precise else [pltpu.VMEM((d, tn), BF16)]
    return pl.pallas_call(
        functools.partial(_merge2_body, precise=precise),
        out_shape=jax.ShapeDtypeStruct((m, d), F32),
        grid=(d // tn, m // tm),
        in_specs=[pl.BlockSpec((tm, d), lambda j, i: (i, 0)),
                  pl.BlockSpec((tm, tn), lambda j, i: (i, j)),
                  pl.BlockSpec((1, 1, tn), lambda j, i: (mod_row0 + i // tiles_per_seq, 0, gate_chunk0 + j)),
                  pl.BlockSpec((d, tn), lambda j, i: (0, j))],
        out_specs=pl.BlockSpec((tm, tn), lambda j, i: (i, j)),
        scratch_shapes=scratch,
        compiler_params=_params(2),
        name="merge_out",
    )(g, x2d, mods3, w_o)


def _route_body(x_ref, g_ref, shift_ref, scale_ref, wr_ref, br_ref, h_ref, idx_ref, wgt_ref, *,
                n_experts):
    h = _rms(x_ref[...]) * g_ref[...] * (1.0 + scale_ref[0]) + shift_ref[0]
    tm, d = h.shape
    lane = lax.broadcasted_iota(jnp.int32, (tm, LANES), 1)
    logits = _dot(h, wr_ref[...], True) + br_ref[...]
    work = jnp.where(lane < n_experts, logits, -jnp.inf)
    vals, idxs = [], []
    for _ in range(TOP_K):
        top = jnp.max(work, axis=-1, keepdims=True)
        sel = jnp.min(jnp.where(work == top, lane, LANES), axis=-1, keepdims=True)
        vals.append(top)
        idxs.append(sel)
        work = jnp.where(lane == sel, -jnp.inf, work)
    exps = [jnp.exp(v - vals[0]) for v in vals]
    denom = exps[0] + exps[1] + exps[2] + exps[3]
    idx_out = jnp.zeros((tm, LANES), jnp.int32)
    wgt_out = jnp.zeros((tm, LANES), F32)
    for k in range(TOP_K):
        idx_out = jnp.where(lane == k, idxs[k], idx_out)
        wgt_out = jnp.where(lane == k, exps[k] / denom, wgt_out)
    idx_ref[...] = idx_out
    wgt_ref[...] = wgt_out
    half = d // 2
    hb = h.astype(BF16).astype(F32)
    lo = lax.shift_right_logical(lax.bitcast_convert_type(hb[:, :half], jnp.uint32), jnp.uint32(16))
    hi = lax.bitcast_convert_type(hb[:, half:], jnp.uint32) & jnp.uint32(0xFFFF0000)
    h_ref[...] = hi | lo


def _route(x1, g, mods3, mod_row0, rows_per_seq, w_router_pad, b_router_pad, n_experts, tm,
           total_rows, row0, prev):
    m, d = x1.shape
    tiles_per_seq = rows_per_seq // tm
    n_steps = m // tm
    pad_steps = 0 if prev is not None else -(-(total_rows - row0 - m) // tm)
    step = lambda i: jnp.minimum(i, n_steps - 1)
    row = lambda i: mod_row0 + step(i) // tiles_per_seq
    blk0 = row0 // tm
    out_shapes = [jax.ShapeDtypeStruct((total_rows, d // 2), jnp.uint32),
                  jax.ShapeDtypeStruct((total_rows, LANES), jnp.int32),
                  jax.ShapeDtypeStruct((total_rows, LANES), F32)]
    in_specs = [pl.BlockSpec((tm, d), lambda i: (step(i), 0)),
                pl.BlockSpec((1, d), lambda i: (0, 0)),
                pl.BlockSpec((1, 1, d), lambda i: (row(i), 0, 3)),
                pl.BlockSpec((1, 1, d), lambda i: (row(i), 0, 4)),
                pl.BlockSpec((d, LANES), lambda i: (0, 0)),
                pl.BlockSpec((1, LANES), lambda i: (0, 0))]
    args = [x1, g.reshape(1, d), mods3, mods3, w_router_pad, b_router_pad]
    route = functools.partial(_route_body, n_experts=n_experts)
    aliases = {}
    if prev is not None:
        in_specs += [pl.BlockSpec(memory_space=pl.ANY)] * 3
        args += list(prev)
        aliases = {6: 0, 7: 1, 8: 2}
        body = lambda *refs: route(*refs[:6], *refs[9:])
    else:
        def body(*refs):
            @pl.when(pl.program_id(0) < n_steps)
            def _():
                route(*refs)

            @pl.when(pl.program_id(0) >= n_steps)
            def _():
                for o_ref in refs[6:]:
                    o_ref[...] = jnp.zeros(o_ref.shape, o_ref.dtype)
    return pl.pallas_call(
        body,
        out_shape=out_shapes,
        grid=(n_steps + pad_steps,),
        in_specs=in_specs,
        out_specs=[pl.BlockSpec((tm, d // 2), lambda i: (blk0 + i, 0)),
                   pl.BlockSpec((tm, LANES), lambda i: (blk0 + i, 0)),
                   pl.BlockSpec((tm, LANES), lambda i: (blk0 + i, 0))],
        input_output_aliases=aliases,
        compiler_params=_params(1),
        name="norm2_route",
    )(*args)


def _dispatch_body(pos_ref, zrow_ref, h_ref, xs_ref, zeros_sc, sem_zero, sem_rows, *, tt, n_experts,
                   n_tiles):
    i = pl.program_id(0)

    def zero_copy(e):
        dst = xs_ref.at[pl.ds(pl.multiple_of(zrow_ref[e], EXPERT_TILE), EXPERT_TILE)]
        return pltpu.make_async_copy(zeros_sc, dst, sem_zero)

    def tail_copy(tile):
        dst = xs_ref.at[pl.ds(pl.multiple_of(tile * EXPERT_TILE, EXPERT_TILE), EXPERT_TILE)]
        return pltpu.make_async_copy(zeros_sc, dst, sem_zero)

    @pl.when(i == 0)
    def _():
        zeros_sc[...] = jnp.zeros(zeros_sc.shape, zeros_sc.dtype)
        for e in range(n_experts):
            @pl.when(zrow_ref[e] >= 0)
            def _(e=e):
                zero_copy(e).start()
        for e in range(n_experts):
            @pl.when(zrow_ref[e] >= 0)
            def _(e=e):
                zero_copy(e).wait()
        used_tiles = zrow_ref[n_experts]

        def start_tail(tile, carry):
            tail_copy(tile).start()
            return carry

        def wait_tail(tile, carry):
            tail_copy(tile).wait()
            return carry

        lax.fori_loop(used_tiles, n_tiles, start_tail, 0)
        lax.fori_loop(used_tiles, n_tiles, wait_tail, 0)

    base = i * (tt * TOP_K)

    def row_copy(t, slot):
        return pltpu.make_async_copy(h_ref.at[pl.ds(t, 1)], xs_ref.at[pl.ds(slot, 1)], sem_rows)

    def issue(t, carry):
        for k in range(TOP_K):
            row_copy(t, pos_ref[base + t * TOP_K + k]).start()
        return carry

    def drain(t, carry):
        for k in range(TOP_K):
            row_copy(0, 0).wait()
        return carry

    lax.fori_loop(0, tt, issue, 0)
    lax.fori_loop(0, tt, drain, 0)


def _dispatch(h_packed, pos_flat, zrow, n_rows_sorted, n_experts):
    t, w = h_packed.shape
    tt = DISPATCH_TILE
    grid_spec = pltpu.PrefetchScalarGridSpec(
        num_scalar_prefetch=2,
        grid=(t // tt,),
        in_specs=[pl.BlockSpec((tt, w), lambda i, pos, zr: (i, 0))],
        out_specs=pl.BlockSpec(memory_space=pl.ANY),
        scratch_shapes=[pltpu.VMEM((EXPERT_TILE, w), jnp.uint32),
                        pltpu.SemaphoreType.DMA, pltpu.SemaphoreType.DMA])
    return pl.pallas_call(
        functools.partial(_dispatch_body, tt=tt, n_experts=n_experts,
                          n_tiles=n_rows_sorted // EXPERT_TILE),
        out_shape=jax.ShapeDtypeStruct((n_rows_sorted, w), jnp.uint32),
        grid_spec=grid_spec,
        compiler_params=_params(1),
        name="moe_dispatch",
    )(pos_flat, zrow, h_packed)


def _expert_changed(te_ref, i):
    return (i == 0) | (te_ref[i] != te_ref[jnp.maximum(i - 1, 0)])


def _moe_up_body(te_ref, tv_ref, xs_ref, wg_ref, wl_ref, bg_ref, bl_ref, o_ref, w_sc):
    i = pl.program_id(1)

    @pl.when(_expert_changed(te_ref, i))
    def _():
        w_sc[0] = wg_ref[...].astype(BF16)
        w_sc[1] = wl_ref[...].astype(BF16)

    @pl.when(tv_ref[i] > 0)
    def _():
        packed = xs_ref[...]
        half = packed.shape[1]
        lo = lax.bitcast_convert_type(lax.shift_left(packed, jnp.uint32(16)), F32).astype(BF16)
        hi = lax.bitcast_convert_type(packed & jnp.uint32(0xFFFF0000), F32).astype(BF16)
        zg = _dot(lo, w_sc[0, 0:half, :], False) + _dot(hi, w_sc[0, half:, :], False) + bg_ref[...]
        zl = _dot(lo, w_sc[1, 0:half, :], False) + _dot(hi, w_sc[1, half:, :], False) + bl_ref[...]
        zg = jnp.minimum(zg, SWIGLU_LIMIT)
        zl = jnp.clip(zl, -SWIGLU_LIMIT, SWIGLU_LIMIT)
        o_ref[...] = (zg * jax.nn.sigmoid(SWIGLU_ALPHA * zg) * (zl + 1.0)).astype(o_ref.dtype)

    @pl.when(tv_ref[i] == 0)
    def _():
        o_ref[...] = jnp.zeros(o_ref.shape, o_ref.dtype)


def _moe_up(xs, tile_expert, tile_valid, w1, b1, fc):
    p, half = xs.shape
    n_e, d, two_f = w1.shape
    f = two_f // 2
    n_tiles = p // EXPERT_TILE
    nc = f // fc
    grid_spec = pltpu.PrefetchScalarGridSpec(
        num_scalar_prefetch=2,
        grid=(nc, n_tiles),
        in_specs=[pl.BlockSpec((EXPERT_TILE, half), lambda c, i, te, tv: (i, 0)),
                  pl.BlockSpec((None, d, fc), lambda c, i, te, tv: (te[i], 0, c)),
                  pl.BlockSpec((None, d, fc), lambda c, i, te, tv: (te[i], 0, nc + c)),
                  pl.BlockSpec((None, 1, fc), lambda c, i, te, tv: (te[i], 0, c)),
                  pl.BlockSpec((None, 1, fc), lambda c, i, te, tv: (te[i], 0, nc + c))],
        out_specs=pl.BlockSpec((EXPERT_TILE, fc), lambda c, i, te, tv: (i, c)),
        scratch_shapes=[pltpu.VMEM((2, d, fc), BF16)])
    b1r = b1.reshape(n_e, 1, two_f)
    return pl.pallas_call(
        _moe_up_body,
        out_shape=jax.ShapeDtypeStruct((p, f), BF16),
        grid_spec=grid_spec,
        compiler_params=_params(2),
        name="moe_up",
    )(tile_expert, tile_valid, xs, w1, w1, b1r, b1r)


def _moe_down_body(te_ref, tv_ref, a_ref, w_ref, b_ref, o_ref, w_sc):
    i = pl.program_id(1)

    @pl.when(_expert_changed(te_ref, i))
    def _():
        w_sc[...] = w_ref[...].astype(BF16)

    @pl.when(tv_ref[i] > 0)
    def _():
        o_ref[...] = _dot(a_ref[...], w_sc[...], False) + b_ref[...]

    @pl.when(tv_ref[i] == 0)
    def _():
        o_ref[...] = jnp.zeros(o_ref.shape, o_ref.dtype)


def _moe_down(act, tile_expert, tile_valid, w2, b2, tn):
    p, f = act.shape
    n_e, _, d = w2.shape
    n_tiles = p // EXPERT_TILE
    grid_spec = pltpu.PrefetchScalarGridSpec(
        num_scalar_prefetch=2,
        grid=(d // tn, n_tiles),
        in_specs=[pl.BlockSpec((EXPERT_TILE, f), lambda c, i, te, tv: (i, 0)),
                  pl.BlockSpec((None, f, tn), lambda c, i, te, tv: (te[i], 0, c)),
                  pl.BlockSpec((None, 1, tn), lambda c, i, te, tv: (te[i], 0, c))],
        out_specs=pl.BlockSpec((EXPERT_TILE, tn), lambda c, i, te, tv: (i, c)),
        scratch_shapes=[pltpu.VMEM((f, tn), BF16)])
    return pl.pallas_call(
        _moe_down_body,
        out_shape=jax.ShapeDtypeStruct((p, d), F32),
        grid_spec=grid_spec,
        compiler_params=_params(2),
        name="moe_down",
    )(tile_expert, tile_valid, act, w2, b2.reshape(n_e, 1, d))


def _combine_body(pos_ref, x_ref, wgt_ref, gate_ref, gf_ref, ys_ref, o_ref, buf_sc, sem, *, tt, tok0):
    i = pl.program_id(0)
    base = (tok0 + i * tt) * TOP_K

    def row_copy(t, k, slot):
        return pltpu.make_async_copy(ys_ref.at[pl.ds(slot, 1)], buf_sc.at[k, pl.ds(t, 1)], sem)

    def issue(t, carry):
        for k in range(TOP_K):
            row_copy(t, k, pos_ref[base + t * TOP_K + k]).start()
        return carry

    def drain(t, carry):
        for k in range(TOP_K):
            row_copy(0, k, 0).wait()
        return carry

    lax.fori_loop(0, tt, issue, 0)
    lax.fori_loop(0, tt, drain, 0)
    wgt = wgt_ref[...]
    moe = buf_sc[0] * wgt[:, 0:1]
    for k in range(1, TOP_K):
        moe = moe + buf_sc[k] * wgt[:, k:k + 1]
    x2 = x_ref[...] + gate_ref[0] * moe
    o_ref[...] = _rms(x2) * gf_ref[...]


def _combine(x1, wgt, pos_flat, ys, mods3, mod_row0, rows_per_seq, g_final, tt, tok0):
    m, d = x1.shape
    tiles_per_seq = rows_per_seq // tt
    wblk0 = tok0 // tt
    grid_spec = pltpu.PrefetchScalarGridSpec(
        num_scalar_prefetch=1,
        grid=(m // tt,),
        in_specs=[pl.BlockSpec((tt, d), lambda i, pos: (i, 0)),
                  pl.BlockSpec((tt, LANES), lambda i, pos: (wblk0 + i, 0)),
                  pl.BlockSpec((1, 1, d), lambda i, pos: (mod_row0 + i // tiles_per_seq, 0, 5)),
                  pl.BlockSpec((1, d), lambda i, pos: (0, 0)),
                  pl.BlockSpec(memory_space=pl.ANY)],
        out_specs=pl.BlockSpec((tt, d), lambda i, pos: (i, 0)),
        scratch_shapes=[pltpu.VMEM((TOP_K, tt, d), F32), pltpu.SemaphoreType.DMA])
    return pl.pallas_call(
        functools.partial(_combine_body, tt=tt, tok0=tok0),
        out_shape=jax.ShapeDtypeStruct((m, d), F32),
        grid_spec=grid_spec,
        compiler_params=_params(1),
        name="moe_combine",
    )(pos_flat, x1, wgt, mods3, g_final.reshape(1, d), ys)


def _routing_tables(idx, n_experts, n_tiles):
    t = idx.shape[0]
    onehot = (idx[:, :, None] == jnp.arange(n_experts, dtype=jnp.int32)).astype(jnp.int32)
    per_tok = onehot.sum(axis=1)
    before = jnp.cumsum(per_tok, axis=0) - per_tok
    counts = per_tok.sum(axis=0)
    padded = (counts + EXPERT_TILE - 1) // EXPERT_TILE * EXPERT_TILE
    ends = jnp.cumsum(padded)
    starts = ends - padded
    pos = (onehot * (before + starts)[:, None, :]).sum(axis=-1)
    tile_start = jnp.arange(n_tiles, dtype=jnp.int32) * EXPERT_TILE
    tile_expert = jnp.minimum((tile_start[:, None] >= ends[None, :]).sum(axis=1), n_experts - 1)
    tile_valid = (tile_start < ends[-1]).astype(jnp.int32)
    last_expert = jnp.max(jnp.where(counts > 0, jnp.arange(n_experts, dtype=jnp.int32), 0))
    tile_expert = jnp.where(tile_valid > 0, tile_expert, last_expert).astype(jnp.int32)
    zrow = jnp.where(padded > 0, ends - EXPERT_TILE, -1)
    zrow = jnp.concatenate([zrow, ends[-1:] // EXPERT_TILE]).astype(jnp.int32)
    return pos.reshape(t * TOP_K).astype(jnp.int32), tile_expert, tile_valid, zrow


def kernel(x_prompt, x_sample, cache_kv_w128, cache_kv_w512, cache_kv_w2048, state_conv, c_prompt, c_sample, w_ada, b_ada, g_norm1, w_in, b_in, w_dw, b_dw, g_ln_conv, b_ln_conv, w_conv_out, b_conv_out, w_attn_out, b_attn_out, w_o, g_norm2, w_router, b_router, w_moe1, b_moe1, w_moe2, b_moe2, g_final):
    caches = (cache_kv_w128, cache_kv_w512, cache_kv_w2048)
    n_layers = w_ada.shape[0]
    assert n_layers == 1, "one layer per step"
    batch, seq, d = x_prompt.shape
    n_seq, n_new, _ = x_sample.shape
    n_heads, head_dim = cache_kv_w128.shape[-2:]
    d_conv = state_conv.shape[-1]
    n_experts = w_router.shape[-1]
    attn_w = N_GROUPS * n_heads * head_dim
    n_p, n_s = batch * seq, n_seq * n_new
    n_tok = n_p + n_s
    lyr = 0

    n_c = batch + n_seq
    c_rows = jnp.concatenate([c_prompt, c_sample, jnp.zeros((-n_c % 8, d), F32)], axis=0)
    mods = _adaln(c_rows, w_ada[lyr], b_ada[lyr])
    mods3 = mods.reshape(mods.shape[0], 1, N_MOD * d)

    xp2 = x_prompt.reshape(n_p, d)
    xs2 = x_sample.reshape(n_s, d)
    w_in_l, b_in_l = w_in[lyr], b_in[lyr].reshape(1, -1)

    def sublayer1(x2, mod_row0, rows_per_seq, tm_norm, tm, precise):
        act_dtype = F32 if precise else BF16
        h = _prenorm(x2, g_norm1[lyr], mods3, mod_row0, rows_per_seq, tm_norm, 0, act_dtype)
        proj = functools.partial(_inproj, h, w_in_l, b_in_l, tm=tm, precise=precise)
        z_qkv = proj(0, 3 * attn_w, "linear", F32, tn=512)
        u = proj(3 * attn_w, d_conv, "glu", F32, tn=512)
        gates = proj(3 * attn_w + 2 * d_conv, 2 * d, "sigmoid", act_dtype, tn=512)
        return z_qkv, u, gates

    zp, up, gates_p = sublayer1(xp2, 0, seq, 512, 512, False)
    zs, us, gates_s = sublayer1(xs2, batch, n_new, n_new, n_s, True)

    attn_p = _attn_prompt(zp, batch, seq, n_heads, head_dim)
    attn_s = _attn_sample(zs, [c[lyr] for c in caches], n_seq, n_new, n_heads, head_dim)

    conv_args = (w_dw[lyr], b_dw[lyr], g_ln_conv[lyr], b_ln_conv[lyr])
    up3 = up.reshape(batch, seq, d_conv)
    us3 = us.reshape(n_seq, n_new, d_conv)
    conv_p = _conv_branch(up3, up3, *conv_args, tt=256, prev_from_u=True, out_dtype=BF16)
    state_pad = jnp.pad(state_conv[lyr], ((0, 0), (CONV_HALO - (CONV_K - 1), 0), (0, 0)))
    conv_s = _conv_branch(us3, state_pad, *conv_args, tt=n_new, prev_from_u=False, out_dtype=F32)

    def merge(attn_o, conv_f, gates, x2, mod_row0, rows_per_seq, tm1, tm2, precise):
        g = _merge1(attn_o, conv_f, gates, w_attn_out[lyr], b_attn_out[lyr], w_conv_out[lyr],
                    b_conv_out[lyr], tm1, 512, precise, F32 if precise else BF16)
        return _merge2(g, x2, mods3, mod_row0, rows_per_seq, w_o[lyr], tm2, 512, precise)

    xp1 = merge(attn_p, conv_p.reshape(n_p, d_conv), gates_p, xp2, 0, seq, 512, 512, False)
    xs1 = merge(attn_s, conv_s.reshape(n_s, d_conv), gates_s, xs2, batch, n_new, n_s, n_new, True)

    w_r = jnp.pad(w_router[lyr], ((0, 0), (0, LANES - n_experts)))
    b_r = jnp.pad(b_router[lyr], (0, LANES - n_experts)).reshape(1, LANES)
    bufs = _route(xp1, g_norm2[lyr], mods3, 0, seq, w_r, b_r, n_experts, 256, n_tok, 0, None)
    h_packed, idx_all, wgt_all = _route(xs1, g_norm2[lyr], mods3, batch, n_new, w_r, b_r, n_experts,
                                        n_new, n_tok, n_p, bufs)

    n_tiles = -(-(n_tok * TOP_K + n_experts * (EXPERT_TILE - 1)) // EXPERT_TILE)
    pos_flat, tile_expert, tile_valid, zrow = _routing_tables(idx_all[:, :TOP_K], n_experts, n_tiles)
    xs_sorted = _dispatch(h_packed, pos_flat, zrow, n_tiles * EXPERT_TILE, n_experts)
    act = _moe_up(xs_sorted, tile_expert, tile_valid, w_moe1[lyr], b_moe1[lyr], fc=512)
    ys = _moe_down(act, tile_expert, tile_valid, w_moe2[lyr], b_moe2[lyr], tn=1024)

    y_prompt = _combine(xp1, wgt_all, pos_flat, ys, mods3, 0, seq, g_final, 128, 0)
    y_sample = _combine(xs1, wgt_all, pos_flat, ys, mods3, batch, n_new, g_final, n_new, n_p)

    def kv_rows(z, nb, length, g, keep):
        k = z[:, attn_w + g * n_heads * head_dim:attn_w + (g + 1) * n_heads * head_dim]
        v = z[:, 2 * attn_w + g * n_heads * head_dim:2 * attn_w + (g + 1) * n_heads * head_dim]
        kv = jnp.stack([k.reshape(nb, length, n_heads, head_dim),
                        v.reshape(nb, length, n_heads, head_dim)], axis=2)
        return kv[:, length - keep:][None]

    kv_p = [kv_rows(zp, batch, seq, g, min(w, seq)) for g, (w, _) in enumerate(DILATED_GROUPS)]
    kv_s = [kv_rows(zs, n_seq, n_new, g, n_new) for g in range(N_GROUPS)]
    conv_state_p = up3[:, seq - (CONV_K - 1):][None]
    conv_state_s = jnp.concatenate([state_conv[lyr], us3], axis=1)[:, n_new:][None]
    return (y_prompt.reshape(batch, seq, d), y_sample.reshape(n_seq, n_new, d),
            kv_p[0], kv_p[1], kv_p[2], conv_state_p, kv_s[0], kv_s[1], kv_s[2], conv_state_s)
```
